```python
import math
import jax, jax.numpy as jnp
from jax import lax
import numpy as np

D_MODEL = 4096
BATCH = 4
SEQ = 2048
DEPTH = 1

MEM_LEN = 256
MLA_HEADS = 16
MLA_Q_RANK = 1024
MLA_KV_RANK = 512
MLA_NOPE = 128
MLA_ROPE = 64
MLA_V = 128
Q_BLOCK = 128
RET_HEADS = 8
RET_DK = 256
RET_DV = 512
RET_CHUNK = 128
X_HEADS = 4
X_HEAD_DIM = D_MODEL // X_HEADS
D_FF = 11008
CONV_W = 3
ROPE_THETA = 10000.0
LN_EPS = 1e-5
RMS_EPS = 1e-6
ALPHA = (2 * DEPTH) ** 0.25
BETA = (8 * DEPTH) ** -0.25
N_BRANCHES = 2

IN_SIZES = (MLA_Q_RANK, MLA_KV_RANK, MLA_ROPE,
            RET_HEADS * RET_DK, RET_HEADS * RET_DK, RET_HEADS * RET_DV, RET_HEADS * RET_DV,
            N_BRANCHES * D_MODEL)
IN_WIDTH = int(sum(IN_SIZES))
IN_SPLITS = [int(v) for v in np.cumsum(IN_SIZES)[:-1]]

kernel_name = 'hybrid_mla_retention_encoder_block'


def _layer_norm(x, g, b):
    xf = x.astype(jnp.float32)
    mu = jnp.mean(xf, axis=-1, keepdims=True)
    var = jnp.mean(jnp.square(xf - mu), axis=-1, keepdims=True)
    y = (xf - mu) * lax.rsqrt(var + LN_EPS)
    return (y * g.astype(jnp.float32) + b.astype(jnp.float32)).astype(x.dtype)


def _rms_norm(x, g):
    xf = x.astype(jnp.float32)
    y = xf * lax.rsqrt(jnp.mean(jnp.square(xf), axis=-1, keepdims=True) + RMS_EPS)
    return (y * g.astype(jnp.float32)).astype(x.dtype)


def _group_norm(o):
    of = o.astype(jnp.float32)
    mu = jnp.mean(of, axis=-1, keepdims=True)
    var = jnp.mean(jnp.square(of - mu), axis=-1, keepdims=True)
    return ((of - mu) * lax.rsqrt(var + LN_EPS)).astype(o.dtype)


def _rope(t, positions):
    d = t.shape[-1]
    half = d // 2
    inv = ROPE_THETA ** (-jnp.arange(half, dtype=jnp.float32) / half)
    ang = positions.astype(jnp.float32)[:, :, None] * inv
    cos = jnp.cos(ang)[:, :, None, :].astype(t.dtype)
    sin = jnp.sin(ang)[:, :, None, :].astype(t.dtype)
    t1, t2 = t[..., :half], t[..., half:]
    return jnp.concatenate([t1 * cos - t2 * sin, t2 * cos + t1 * sin], axis=-1)


def _blocked_attention(q, k, v, scale):
    B, S, H, dq = q.shape
    dv = v.shape[-1]
    nb = S // Q_BLOCK
    qb = q.reshape(B, nb, Q_BLOCK, H, dq).transpose(1, 0, 2, 3, 4)

    def one_block(q_blk):
        s = jnp.einsum('bqhd,bkhd->bhqk', q_blk, k).astype(jnp.float32) * scale
        p = jax.nn.softmax(s, axis=-1).astype(v.dtype)
        return jnp.einsum('bhqk,bkhe->bqhe', p, v)

    o = lax.map(one_block, qb)
    return o.transpose(1, 0, 2, 3, 4).reshape(B, S, H, dv)


def _log_decay(e):
    return jnp.log1p(-jnp.exp2(-e.astype(jnp.float32)))


def _retention_scan(q, k, v, log_g, strict):
    B, H, S, dk = q.shape
    dv = v.shape[-1]
    nc = S // RET_CHUNK
    dt = q.dtype
    idx = jnp.arange(RET_CHUNK, dtype=jnp.float32)
    diff = idx[:, None] - idx[None, :]
    mask = (diff > 0) if strict else (diff >= 0)
    decay_in = jnp.where(mask[None], jnp.exp(log_g[:, None, None] * jnp.where(mask, diff, 0.0)[None]), 0.0).astype(dt)
    xi = jnp.exp(log_g[:, None] * (idx + 1.0)).astype(dt)[None, :, :, None]
    zeta = jnp.exp(log_g[:, None] * (RET_CHUNK - 1.0 - idx)).astype(dt)[None, :, :, None]
    g_chunk = jnp.exp(log_g * RET_CHUNK).astype(dt)[None, :, None, None]

    def to_chunks(t):
        return jnp.moveaxis(t.reshape(B, H, nc, RET_CHUNK, t.shape[-1]), 2, 0)

    def step(state, inp):
        qc, kc, vc = inp
        scores = jnp.einsum('bhnd,bhmd->bhnm', qc, kc) * decay_in[None]
        o = jnp.einsum('bhnm,bhme->bhne', scores, vc) + jnp.einsum('bhnd,bhde->bhne', qc, state) * xi
        state = state * g_chunk + jnp.einsum('bhmd,bhme->bhde', kc * zeta, vc)
        return state, o

    state0 = jnp.zeros((B, H, dk, dv), dtype=dt)
    _, o = lax.scan(step, state0, (to_chunks(q), to_chunks(k), to_chunks(v)))
    return jnp.moveaxis(o, 0, 2).reshape(B, H, S, dv)


def _dwconv_centred(u, w, b):
    up = jnp.pad(u, ((0, 0), (1, 1), (0, 0)))
    return up[:, :-2] * w[0] + up[:, 1:-1] * w[1] + up[:, 2:] * w[2] + b


def setup_inputs(seed: int = 0) -> dict:
    key = jax.random.key(seed)
    ks = jax.random.split(key, 32)
    f32 = jnp.float32

    def w(k, shape, fan_in, scale=1.0):
        return jax.random.normal(k, shape, f32) * (fan_in ** -0.5) * scale

    def gain(k, shape):
        return 1.0 + 0.02 * jax.random.normal(k, shape, f32)

    def bias(k, shape):
        return 0.01 * jax.random.normal(k, shape, f32)

    L = DEPTH
    x = jax.random.normal(ks[0], (BATCH, SEQ, D_MODEL), f32)
    mem = jax.random.normal(ks[1], (BATCH, MEM_LEN, D_MODEL), f32)
    positions = jax.random.randint(ks[2], (BATCH, 1), 0, 1024, dtype=jnp.int32) + jnp.arange(SEQ, dtype=jnp.int32)[None, :]
    decay_base = 5.0 + jnp.arange(RET_HEADS, dtype=f32)
    return {
        'x': x,
        'mem': mem,
        'positions': positions,
        'w_in': w(ks[3], (L, D_MODEL, IN_WIDTH), D_MODEL),
        'gate_bias': bias(ks[4], (L, N_BRANCHES, D_MODEL)),
        'q_norm_g': gain(ks[5], (L, MLA_Q_RANK)),
        'w_uq': w(ks[6], (L, MLA_Q_RANK, MLA_HEADS * (MLA_NOPE + MLA_ROPE)), MLA_Q_RANK),
        'kv_norm_g': gain(ks[7], (L, MLA_KV_RANK)),
        'w_ukv': w(ks[8], (L, MLA_KV_RANK, MLA_HEADS * (MLA_NOPE + MLA_V)), MLA_KV_RANK),
        'ret_decay_fwd': decay_base + 0.1 * jax.random.normal(ks[9], (L, RET_HEADS), f32),
        'ret_decay_bwd': decay_base + 0.1 * jax.random.normal(ks[10], (L, RET_HEADS), f32),
        'w_br_mla': w(ks[11], (L, MLA_HEADS * MLA_V, D_MODEL), MLA_HEADS * MLA_V),
        'w_br_ret': w(ks[12], (L, RET_HEADS * RET_DV, D_MODEL), RET_HEADS * RET_DV),
        'w_o': w(ks[13], (L, D_MODEL, D_MODEL), D_MODEL, BETA),
        'ln1_g': gain(ks[14], (L, D_MODEL)),
        'ln1_b': bias(ks[15], (L, D_MODEL)),
        'w_cq': w(ks[16], (L, D_MODEL, D_MODEL), D_MODEL),
        'w_ck': w(ks[17], (L, D_MODEL, D_MODEL), D_MODEL),
        'w_cv': w(ks[18], (L, D_MODEL, D_MODEL), D_MODEL),
        'w_co': w(ks[19], (L, D_MODEL, D_MODEL), D_MODEL, BETA),
        'ln2_g': gain(ks[20], (L, D_MODEL)),
        'ln2_b': bias(ks[21], (L, D_MODEL)),
        'w_ffn_in': w(ks[22], (L, D_MODEL, 2 * D_FF), D_MODEL),
        'conv_w': w(ks[23], (L, CONV_W, 2 * D_FF), CONV_W),
        'conv_b': bias(ks[24], (L, 2 * D_FF)),
        'w_ffn_out': w(ks[25], (L, D_FF, D_MODEL), D_FF, BETA),
        'ln3_g': gain(ks[26], (L, D_MODEL)),
        'ln3_b': bias(ks[27], (L, D_MODEL)),
    }


def reference(x, mem, positions, w_in, gate_bias, q_norm_g, w_uq, kv_norm_g, w_ukv,
              ret_decay_fwd, ret_decay_bwd, w_br_mla, w_br_ret, w_o, ln1_g, ln1_b,
              w_cq, w_ck, w_cv, w_co, ln2_g, ln2_b, w_ffn_in, conv_w, conv_b,
              w_ffn_out, ln3_g, ln3_b):
    B, S, _ = x.shape
    M = mem.shape[1]
    h = x
    for l in range(DEPTH):
        proj = h @ w_in[l]
        c_q, c_kv, k_r, r_q, r_k, r_v, r_g, gates = jnp.split(proj, IN_SPLITS, axis=-1)

        q = (_rms_norm(c_q, q_norm_g[l]) @ w_uq[l]).reshape(B, S, MLA_HEADS, MLA_NOPE + MLA_ROPE)
        q_nope, q_pe = q[..., :MLA_NOPE], _rope(q[..., MLA_NOPE:], positions)
        kv = (_rms_norm(c_kv, kv_norm_g[l]) @ w_ukv[l]).reshape(B, S, MLA_HEADS, MLA_NOPE + MLA_V)
        k_nope, v_mla = kv[..., :MLA_NOPE], kv[..., MLA_NOPE:]
        k_pe = _rope(k_r[:, :, None, :], positions)
        q_full = jnp.concatenate([q_nope, q_pe], axis=-1)
        k_full = jnp.concatenate([k_nope, jnp.broadcast_to(k_pe, (B, S, MLA_HEADS, MLA_ROPE))], axis=-1)
        a_out = _blocked_attention(q_full, k_full, v_mla, (MLA_NOPE + MLA_ROPE) ** -0.5)
        a_out = a_out.reshape(B, S, MLA_HEADS * MLA_V)

        rq = _rope(r_q.reshape(B, S, RET_HEADS, RET_DK), positions).transpose(0, 2, 1, 3)
        rk = (_rope(r_k.reshape(B, S, RET_HEADS, RET_DK), positions) * (RET_DK ** -0.5)).transpose(0, 2, 1, 3)
        rv = r_v.reshape(B, S, RET_HEADS, RET_DV).transpose(0, 2, 1, 3)
        o_fwd = _retention_scan(rq, rk, rv, _log_decay(ret_decay_fwd[l]), False)
        o_bwd = _retention_scan(rq[:, :, ::-1], rk[:, :, ::-1], rv[:, :, ::-1],
                                _log_decay(ret_decay_bwd[l]), True)[:, :, ::-1]
        o_ret = _group_norm(o_fwd + o_bwd).transpose(0, 2, 1, 3).reshape(B, S, RET_HEADS * RET_DV)
        r_out = jax.nn.silu(r_g) * o_ret

        g = jax.nn.sigmoid(gates.reshape(B, S, N_BRANCHES, D_MODEL) + gate_bias[l])
        mixed = g[:, :, 0] * (a_out @ w_br_mla[l]) + g[:, :, 1] * (r_out @ w_br_ret[l])
        h = _layer_norm(ALPHA * h + mixed @ w_o[l], ln1_g[l], ln1_b[l])

        cq = (h @ w_cq[l]).reshape(B, S, X_HEADS, X_HEAD_DIM)
        ck = (mem @ w_ck[l]).reshape(B, M, X_HEADS, X_HEAD_DIM)
        cv = (mem @ w_cv[l]).reshape(B, M, X_HEADS, X_HEAD_DIM)
        s = jnp.einsum('bqhd,bmhd->bhqm', cq, ck).astype(jnp.float32) * (X_HEAD_DIM ** -0.5)
        p = jax.nn.softmax(s, axis=-1).astype(cv.dtype)
        c_out = jnp.einsum('bhqm,bmhd->bqhd', p, cv).reshape(B, S, D_MODEL)
        h = _layer_norm(ALPHA * h + c_out @ w_co[l], ln2_g[l], ln2_b[l])

        u = _dwconv_centred(h @ w_ffn_in[l], conv_w[l], conv_b[l])
        up, gt = u[..., :D_FF], u[..., D_FF:]
        h = _layer_norm(ALPHA * h + (jax.nn.silu(gt) * up) @ w_ffn_out[l], ln3_g[l], ln3_b[l])
    return h
```

```python
import functools

import jax
import jax.numpy as jnp
from jax import lax
from jax.experimental import pallas as pl
from jax.experimental.pallas import tpu as pltpu

F32 = jnp.float32
BF16 = jnp.bfloat16

DEPTH = 1
MLA_HEADS = 16
MLA_Q_RANK = 1024
MLA_KV_RANK = 512
MLA_NOPE = 128
MLA_ROPE = 64
MLA_V = 128
MLA_QK_PAD = 256
RET_HEADS = 8
RET_DK = 256
RET_DV = 512
RET_CHUNK = 256
X_HEADS = 4
D_FF = 11008
D_FF_PAD = 11264
ROPE_THETA = 10000.0
LN_EPS = 1e-5
RMS_EPS = 1e-6
ALPHA = (2 * DEPTH) ** 0.25

LANES = 128
CONV_HALO = 16
VMEM_LIMIT_BYTES = 56 * 2**20

NT_DIMS = (((1,), (1,)), ((), ()))
TN_DIMS = (((0,), (0,)), ((), ()))


def _params(*semantics):
    return pltpu.CompilerParams(dimension_semantics=semantics, vmem_limit_bytes=VMEM_LIMIT_BYTES)


def _dot(a, b):
    return jnp.dot(a, b, preferred_element_type=F32)


def _mm_body(x_ref, w_ref, o_ref):
    o_ref[...] = _dot(x_ref[...], w_ref[...]).astype(o_ref.dtype)


def _matmul(x, w, out_dtype, bm, bn, name):
    M, K = x.shape
    N = w.shape[1]
    return pl.pallas_call(
        _mm_body,
        out_shape=jax.ShapeDtypeStruct((M, N), out_dtype),
        grid=(M // bm, N // bn),
        in_specs=[pl.BlockSpec((bm, K), lambda m, n: (m, 0)),
                  pl.BlockSpec((K, bn), lambda m, n: (0, n))],
        out_specs=pl.BlockSpec((bm, bn), lambda m, n: (m, n)),
        compiler_params=_params("parallel", "arbitrary"),
        name=name,
    )(x, w)


def _qk_rope_body(x_ref, w_ref, cos_ref, sin_ref, o_ref, *, bn, k_first_tile):
    y = _dot(x_ref[...], w_ref[...])
    cos = cos_ref[...]
    sin = sin_ref[...]
    half = RET_DK // 2
    scale = jnp.where(pl.program_id(1) >= k_first_tile, RET_DK ** -0.5, 1.0).astype(F32)
    for h in range(bn // RET_DK):
        lo = h * RET_DK
        t1 = y[:, lo:lo + half]
        t2 = y[:, lo + half:lo + RET_DK]
        o_ref[:, lo:lo + half] = ((t1 * cos - t2 * sin) * scale).astype(o_ref.dtype)
        o_ref[:, lo + half:lo + RET_DK] = ((t2 * cos + t1 * sin) * scale).astype(o_ref.dtype)


def _qk_rope(x, w, cos, sin, bm, bn):
    M, K = x.shape
    N = w.shape[1]
    half = RET_DK // 2
    body = functools.partial(_qk_rope_body, bn=bn, k_first_tile=(N // 2) // bn)
    return pl.pallas_call(
        body,
        out_shape=jax.ShapeDtypeStruct((M, N), BF16),
        grid=(M // bm, N // bn),
        in_specs=[pl.BlockSpec((bm, K), lambda m, n: (m, 0)),
                  pl.BlockSpec((K, bn), lambda m, n: (0, n)),
                  pl.BlockSpec((bm, half), lambda m, n: (m, 0)),
                  pl.BlockSpec((bm, half), lambda m, n: (m, 0))],
        out_specs=pl.BlockSpec((bm, bn), lambda m, n: (m, n)),
        compiler_params=_params("parallel", "arbitrary"),
        name="ret_qk_rope",
    )(x, w, cos, sin)


def _mla_prep_body(ca_ref, qg_ref, kvg_ref, wuq_ref, wukv_ref, c_ref, s1_ref, s2_ref,
                   q_ref, kv_ref, kpe_ref):
    ca = ca_ref[...]
    cq = ca[:, :MLA_Q_RANK]
    ckv = ca[:, MLA_Q_RANK:MLA_Q_RANK + MLA_KV_RANK]
    kr = ca[:, MLA_Q_RANK + MLA_KV_RANK:]
    c = c_ref[...]
    s1 = s1_ref[...]
    s2 = s2_ref[...]

    def rms(t, g):
        return t * lax.rsqrt(jnp.mean(t * t, axis=-1, keepdims=True) + RMS_EPS) * g

    def rope(t):
        return t * c + pltpu.roll(t, MLA_ROPE // 2, 1) * s1 + pltpu.roll(t, LANES - MLA_ROPE // 2, 1) * s2

    qscale = (MLA_NOPE + MLA_ROPE) ** -0.5
    q = _dot(rms(cq, qg_ref[...]).astype(BF16), wuq_ref[...])
    for h in range(MLA_HEADS):
        lo = h * MLA_QK_PAD
        q_ref[:, lo:lo + MLA_NOPE] = (q[:, lo:lo + MLA_NOPE] * qscale).astype(BF16)
        q_ref[:, lo + MLA_NOPE:lo + MLA_QK_PAD] = (rope(q[:, lo + MLA_NOPE:lo + MLA_QK_PAD]) * qscale).astype(BF16)
    kv_ref[...] = _dot(rms(ckv, kvg_ref[...]).astype(BF16), wukv_ref[...]).astype(BF16)
    kpe_ref[...] = rope(kr).astype(BF16)


def _mla_prep(ca, qg, kvg, wuq, wukv, c, s1, s2, bm):
    M, CA = ca.shape
    NQ = wuq.shape[1]
    NKV = wukv.shape[1]
    const = lambda m: (0, 0)
    row = lambda m: (m, 0)
    return pl.pallas_call(
        _mla_prep_body,
        out_shape=(jax.ShapeDtypeStruct((M, NQ), BF16),
                   jax.ShapeDtypeStruct((M, NKV), BF16),
                   jax.ShapeDtypeStruct((M, LANES), BF16)),
        grid=(M // bm,),
        in_specs=[pl.BlockSpec((bm, CA), row),
                  pl.BlockSpec((1, MLA_Q_RANK), const),
                  pl.BlockSpec((1, MLA_KV_RANK), const),
                  pl.BlockSpec((MLA_Q_RANK, NQ), const),
                  pl.BlockSpec((MLA_KV_RANK, NKV), const),
                  pl.BlockSpec((bm, LANES), row),
                  pl.BlockSpec((bm, LANES), row),
                  pl.BlockSpec((bm, LANES), row)],
        out_specs=(pl.BlockSpec((bm, NQ), row),
                   pl.BlockSpec((bm, NKV), row),
                   pl.BlockSpec((bm, LANES), row)),
        compiler_params=_params("parallel"),
        name="mla_prep",
    )(ca, qg, kvg, wuq, wukv, c, s1, s2)


def _mla_attn_body(q_ref, kn_ref, v_ref, kpe_ref, o_ref, kfull_ref):
    @pl.when(pl.program_id(2) == 0)
    def _():
        kfull_ref[:, :MLA_NOPE] = kn_ref[...]
        kfull_ref[:, MLA_NOPE:] = kpe_ref[...]

    s = lax.dot_general(q_ref[...], kfull_ref[...], NT_DIMS, preferred_element_type=F32)
    p = jnp.exp(s - jnp.max(s, axis=-1, keepdims=True))
    l = jnp.sum(p, axis=-1, keepdims=True)
    o_ref[...] = (_dot(p.astype(BF16), v_ref[...]) / l).astype(o_ref.dtype)


def _mla_attn(q, kv, kpe, B, S, bq):
    nq = S // bq
    return pl.pallas_call(
        _mla_attn_body,
        out_shape=jax.ShapeDtypeStruct((B * S, MLA_HEADS * MLA_V), BF16),
        grid=(B, MLA_HEADS, nq),
        in_specs=[pl.BlockSpec((bq, MLA_QK_PAD), lambda b, h, i: (b * nq + i, h)),
                  pl.BlockSpec((S, MLA_NOPE), lambda b, h, i: (b, 2 * h)),
                  pl.BlockSpec((S, MLA_V), lambda b, h, i: (b, 2 * h + 1)),
                  pl.BlockSpec((S, LANES), lambda b, h, i: (b, 0))],
        out_specs=pl.BlockSpec((bq, MLA_V), lambda b, h, i: (b * nq + i, h)),
        scratch_shapes=[pltpu.VMEM((S, MLA_QK_PAD), BF16)],
        compiler_params=_params("parallel", "parallel", "arbitrary"),
        name="mla_attn",
    )(q, kv, kv, kpe)


def _retention_body(ld_ref, q_ref, k_ref, v_ref, g_ref, o_ref, st_ref, acc_ref, *, seq):
    C = RET_CHUNK
    nc = seq // C
    h = pl.program_id(1)
    lgf = ld_ref[0, h]
    lgb = ld_ref[1, h]
    ii = lax.broadcasted_iota(jnp.int32, (C, C), 0)
    jj = lax.broadcasted_iota(jnp.int32, (C, C), 1)
    d = (ii - jj).astype(F32)
    decay = jnp.where(d >= 0, jnp.exp(lgf * jnp.maximum(d, 0.0)), jnp.exp(lgb * jnp.maximum(-d, 0.0)))
    idx = lax.broadcasted_iota(jnp.int32, (C, 1), 0).astype(F32)
    xi_f = jnp.exp(lgf * (idx + 1.0))
    zeta_f = jnp.exp(lgf * (C - 1.0 - idx))
    xi_b = jnp.exp(lgb * (C - idx))
    zeta_b = jnp.exp(lgb * idx)
    gc_f = jnp.exp(jnp.full((1, RET_DV), C, F32) * lgf)
    gc_b = jnp.exp(jnp.full((1, RET_DV), C, F32) * lgb)

    def rows(c):
        return slice(c * C, (c + 1) * C)

    def state_update(c, zeta, gc, first):
        kz = (k_ref[rows(c), :].astype(F32) * zeta).astype(BF16)
        upd = lax.dot_general(kz, v_ref[rows(c), :], TN_DIMS, preferred_element_type=F32)
        st_ref[...] = upd if first else st_ref[...] * gc + upd

    for c in range(nc):
        qc = q_ref[rows(c), :]
        s = lax.dot_general(qc, k_ref[rows(c), :], NT_DIMS, preferred_element_type=F32) * decay
        o = _dot(s.astype(BF16), v_ref[rows(c), :])
        if c > 0:
            o = o + _dot(qc, st_ref[...].astype(BF16)) * xi_f
        acc_ref[rows(c), :] = o
        if c < nc - 1:
            state_update(c, zeta_f, gc_f, c == 0)
    for c in range(nc - 1, -1, -1):
        if c < nc - 1:
            acc_ref[rows(c), :] += _dot(q_ref[rows(c), :], st_ref[...].astype(BF16)) * xi_b
        if c > 0:
            state_update(c, zeta_b, gc_b, c == nc - 1)
    for c in range(nc):
        o = acc_ref[rows(c), :]
        mu = jnp.mean(o, axis=-1, keepdims=True)
        oc = o - mu
        y = oc * lax.rsqrt(jnp.mean(oc * oc, axis=-1, keepdims=True) + LN_EPS)
        g = g_ref[rows(c), :]
        o_ref[rows(c), :] = (g * jax.nn.sigmoid(g) * y).astype(o_ref.dtype)


def _retention(log_decay, qk, v, g, B, S):
    body = functools.partial(_retention_body, seq=S)
    return pl.pallas_call(
        body,
        out_shape=jax.ShapeDtypeStruct((B * S, RET_HEADS * RET_DV), BF16),
        grid=(B, RET_HEADS),
        in_specs=[pl.BlockSpec(memory_space=pltpu.SMEM),
                  pl.BlockSpec((S, RET_DK), lambda b, h: (b, h)),
                  pl.BlockSpec((S, RET_DK), lambda b, h: (b, RET_HEADS + h)),
                  pl.BlockSpec((S, RET_DV), lambda b, h: (b, h)),
                  pl.BlockSpec((S, RET_DV), lambda b, h: (b, h))],
        out_specs=pl.BlockSpec((S, RET_DV), lambda b, h: (b, h)),
        scratch_shapes=[pltpu.VMEM((RET_DK, RET_DV), F32),
                        pltpu.VMEM((S, RET_DV), F32)],
        compiler_params=_params("parallel", "parallel"),
        name="retention",
    )(log_decay, qk, qk, v, g)


def _merge_body(a_ref, r_ref, wa_ref, wr_ref, g0_ref, g1_ref, b0_ref, b1_ref, o_ref):
    ya = _dot(a_ref[...], wa_ref[...])
    yr = _dot(r_ref[...], wr_ref[...])
    ga = jax.nn.sigmoid(g0_ref[...] + b0_ref[...])
    gr = jax.nn.sigmoid(g1_ref[...] + b1_ref[...])
    o_ref[...] = (ga * ya + gr * yr).astype(o_ref.dtype)


def _merge(a, r, wa, wr, gates, gate_bias, bm, bn):
    M = a.shape[0]
    N = wa.shape[1]
    nn = N // bn
    return pl.pallas_call(
        _merge_body,
        out_shape=jax.ShapeDtypeStruct((M, N), BF16),
        grid=(M // bm, nn),
        in_specs=[pl.BlockSpec((bm, a.shape[1]), lambda m, n: (m, 0)),
                  pl.BlockSpec((bm, r.shape[1]), lambda m, n: (m, 0)),
                  pl.BlockSpec((wa.shape[0], bn), lambda m, n: (0, n)),
                  pl.BlockSpec((wr.shape[0], bn), lambda m, n: (0, n)),
                  pl.BlockSpec((bm, bn), lambda m, n: (m, n)),
                  pl.BlockSpec((bm, bn), lambda m, n: (m, nn + n)),
                  pl.BlockSpec((1, bn), lambda m, n: (0, n)),
                  pl.BlockSpec((1, bn), lambda m, n: (0, nn + n))],
        out_specs=pl.BlockSpec((bm, bn), lambda m, n: (m, n)),
        compiler_params=_params("parallel", "arbitrary"),
        name="branch_merge",
    )(a, r, wa, wr, gates, gates, gate_bias, gate_bias)


def _ln_body(y_ref, r_ref, g_ref, b_ref, *o_refs):
    z = ALPHA * r_ref[...] + y_ref[...]
    mu = jnp.mean(z, axis=-1, keepdims=True)
    zc = z - mu
    o = zc * lax.rsqrt(jnp.mean(zc * zc, axis=-1, keepdims=True) + LN_EPS) * g_ref[...] + b_ref[...]
    for o_ref in o_refs:
        o_ref[...] = o.astype(o_ref.dtype)


def _res_ln(y, resid, g, b, bm, with_bf16):
    M, N = y.shape
    row = lambda m: (m, 0)
    const = lambda m: (0, 0)
    shapes = [jax.ShapeDtypeStruct((M, N), F32)]
    if with_bf16:
        shapes.append(jax.ShapeDtypeStruct((M, N), BF16))
    return pl.pallas_call(
        _ln_body,
        out_shape=tuple(shapes),
        grid=(M // bm,),
        in_specs=[pl.BlockSpec((bm, N), row), pl.BlockSpec((bm, N), row),
                  pl.BlockSpec((1, N), const), pl.BlockSpec((1, N), const)],
        out_specs=tuple(pl.BlockSpec((bm, N), row) for _ in shapes),
        compiler_params=_params("parallel"),
        name="res_ln",
    )(y, resid, g, b)


def _xattn_body(x_ref, w_ref, ck_ref, cv_ref, o_ref, *, scale):
    cq = _dot(x_ref[...], w_ref[...]).astype(BF16)
    s = lax.dot_general(cq, ck_ref[...], NT_DIMS, preferred_element_type=F32) * scale
    p = jnp.exp(s - jnp.max(s, axis=-1, keepdims=True))
    l = jnp.sum(p, axis=-1, keepdims=True)
    o_ref[...] = (_dot(p.astype(BF16), cv_ref[...]) / l).astype(o_ref.dtype)


def _xattn(x, w, ckv, S, mem_len, bm):
    M, K = x.shape
    N = w.shape[1]
    hd = N // X_HEADS
    tiles_per_seq = S // bm
    body = functools.partial(_xattn_body, scale=hd ** -0.5)
    return pl.pallas_call(
        body,
        out_shape=jax.ShapeDtypeStruct((M, N), BF16),
        grid=(M // bm, X_HEADS),
        in_specs=[pl.BlockSpec((bm, K), lambda m, h: (m, 0)),
                  pl.BlockSpec((K, hd), lambda m, h: (0, h)),
                  pl.BlockSpec((mem_len, hd), lambda m, h: (m // tiles_per_seq, h)),
                  pl.BlockSpec((mem_len, hd), lambda m, h: (m // tiles_per_seq, X_HEADS + h))],
        out_specs=pl.BlockSpec((bm, hd), lambda m, h: (m, h)),
        compiler_params=_params("parallel", "arbitrary"),
        name="cross_attn",
    )(x, w, ckv, ckv)


def _ffn_in_body(xm_ref, xp_ref, xn_ref, wu_ref, wg_ref, cwu_ref, cwg_ref, cbu_ref, cbg_ref,
                 o_ref, xe_ref, *, bm, tiles_per_seq):
    H = CONV_HALO

    @pl.when(pl.program_id(1) == 0)
    def _():
        t = pl.program_id(0) % tiles_per_seq
        xe_ref[0:H, :] = jnp.where(t == 0, jnp.zeros_like(xp_ref[...]), xp_ref[...])
        xe_ref[H:H + bm, :] = xm_ref[...]
        xe_ref[H + bm:, :] = jnp.where(t == tiles_per_seq - 1, jnp.zeros_like(xn_ref[...]), xn_ref[...])

    xe = xe_ref[...]
    rows = bm + 2 * H

    def conv_branch(w_ref, cw_ref, cb_ref):
        y = _dot(xe, w_ref[...])
        cw = cw_ref[...]
        prev = pltpu.roll(y, 1, 0)[H:H + bm]
        nxt = pltpu.roll(y, rows - 1, 0)[H:H + bm]
        return prev * cw[0:1] + y[H:H + bm] * cw[1:2] + nxt * cw[2:3] + cb_ref[...]

    up = conv_branch(wu_ref, cwu_ref, cbu_ref)
    gt = conv_branch(wg_ref, cwg_ref, cbg_ref)
    o_ref[...] = (gt * jax.nn.sigmoid(gt) * up).astype(o_ref.dtype)


def _ffn_in(x, w, cw, cb, S, bm, bn):
    M, K = x.shape
    NF = w.shape[1] // 2
    nn = NF // bn
    H = CONV_HALO
    tiles_per_seq = S // bm
    hb = bm // H
    last_hb = M // H - 1
    body = functools.partial(_ffn_in_body, bm=bm, tiles_per_seq=tiles_per_seq)
    return pl.pallas_call(
        body,
        out_shape=jax.ShapeDtypeStruct((M, NF), BF16),
        grid=(M // bm, nn),
        in_specs=[pl.BlockSpec((bm, K), lambda m, n: (m, 0)),
                  pl.BlockSpec((H, K), lambda m, n: (jnp.maximum(m * hb - 1, 0), 0)),
                  pl.BlockSpec((H, K), lambda m, n: (jnp.minimum((m + 1) * hb, last_hb), 0)),
                  pl.BlockSpec((K, bn), lambda m, n: (0, n)),
                  pl.BlockSpec((K, bn), lambda m, n: (0, nn + n)),
                  pl.BlockSpec((3, bn), lambda m, n: (0, n)),
                  pl.BlockSpec((3, bn), lambda m, n: (0, nn + n)),
                  pl.BlockSpec((1, bn), lambda m, n: (0, n)),
                  pl.BlockSpec((1, bn), lambda m, n: (0, nn + n))],
        out_specs=pl.BlockSpec((bm, bn), lambda m, n: (m, n)),
        scratch_shapes=[pltpu.VMEM((bm + 2 * H, K), BF16)],
        compiler_params=_params("parallel", "arbitrary"),
        name="ffn_in",
    )(x, x, x, w, w, cw, cw, cb, cb)


def _ffn_out_body(x_ref, w_ref, o_ref):
    y = _dot(x_ref[...], w_ref[...])

    @pl.when(pl.program_id(1) == 0)
    def _():
        o_ref[...] = y

    @pl.when(pl.program_id(1) > 0)
    def _():
        o_ref[...] += y


def _ffn_out(x, w, bm, bk):
    M, K = x.shape
    N = w.shape[1]
    return pl.pallas_call(
        _ffn_out_body,
        out_shape=jax.ShapeDtypeStruct((M, N), F32),
        grid=(M // bm, K // bk),
        in_specs=[pl.BlockSpec((bm, bk), lambda m, k: (m, k)),
                  pl.BlockSpec((bk, N), lambda m, k: (k, 0))],
        out_specs=pl.BlockSpec((bm, N), lambda m, k: (m, 0)),
        compiler_params=_params("parallel", "arbitrary"),
        name="ffn_out",
    )(x, w)


def _rope_angles(positions, half):
    inv = ROPE_THETA ** (-jnp.arange(half, dtype=F32) / half)
    ang = positions.astype(F32)[:, :, None] * inv
    return jnp.cos(ang).reshape(-1, half), jnp.sin(ang).reshape(-1, half)


def _pad_cols(a, groups, width, new_width):
    lead = a.shape[:-1]
    a = a.reshape(lead + (groups, width))
    a = jnp.pad(a, [(0, 0)] * len(lead) + [(0, 0), (0, new_width - width)])
    return a.reshape(lead + (groups * new_width,))


def kernel(x, mem, positions, w_in, gate_bias, q_norm_g, w_uq, kv_norm_g, w_ukv, ret_decay_fwd, ret_decay_bwd, w_br_mla, w_br_ret, w_o, ln1_g, ln1_b, w_cq, w_ck, w_cv, w_co, ln2_g, ln2_b, w_ffn_in, conv_w, conv_b, w_ffn_out, ln3_g, ln3_b):
    B, S, D = x.shape
    M = B * S
    mem_len = mem.shape[1]

    cos_r, sin_r = _rope_angles(positions, RET_DK // 2)
    cos_m, sin_m = _rope_angles(positions, MLA_ROPE // 2)
    z32 = jnp.zeros_like(cos_m)
    z64 = jnp.zeros((M, LANES - MLA_ROPE), F32)
    rope_c = jnp.concatenate([cos_m, cos_m, z64], axis=1)
    rope_s1 = jnp.concatenate([z32, sin_m, z64], axis=1)
    rope_s2 = jnp.concatenate([-sin_m, z32, z64], axis=1)

    h = x.reshape(M, D)
    hb = h.astype(BF16)
    memb = mem.reshape(B * mem_len, D).astype(BF16)

    for l in range(DEPTH):
        o_a = MLA_Q_RANK + MLA_KV_RANK + MLA_ROPE
        o_qk = o_a + 2 * RET_HEADS * RET_DK
        o_v = o_qk + RET_HEADS * RET_DV
        o_g = o_v + RET_HEADS * RET_DV
        w_a = jnp.pad(w_in[l][:, :o_a], ((0, 0), (0, LANES - MLA_ROPE))).astype(BF16)
        w_qk = w_in[l][:, o_a:o_qk].astype(BF16)
        w_v = w_in[l][:, o_qk:o_v].astype(BF16)
        w_g = w_in[l][:, o_v:o_g].astype(BF16)
        w_gates = w_in[l][:, o_g:].astype(BF16)
        wuq = _pad_cols(w_uq[l], MLA_HEADS, MLA_NOPE + MLA_ROPE, MLA_QK_PAD).astype(BF16)
        wukv = w_ukv[l].astype(BF16)
        log_decay = jnp.stack([jnp.log1p(-jnp.exp2(-ret_decay_fwd[l].astype(F32))),
                               jnp.log1p(-jnp.exp2(-ret_decay_bwd[l].astype(F32)))])
        wfi = _pad_cols(w_ffn_in[l], 2, D_FF, D_FF_PAD).astype(BF16)
        cw = _pad_cols(conv_w[l], 2, D_FF, D_FF_PAD)
        cb = _pad_cols(conv_b[l][None, :], 2, D_FF, D_FF_PAD)
        wfo = jnp.pad(w_ffn_out[l], ((0, D_FF_PAD - D_FF), (0, 0))).astype(BF16)

        ca = _matmul(hb, w_a, F32, 512, w_a.shape[1], "proj_mla")
        q, kv, kpe = _mla_prep(ca, q_norm_g[l][None, :], kv_norm_g[l][None, :], wuq, wukv,
                               rope_c, rope_s1, rope_s2, 256)
        a_out = _mla_attn(q, kv, kpe, B, S, 512)

        rqk = _qk_rope(hb, w_qk, cos_r, sin_r, 1024, 1024)
        rv = _matmul(hb, w_v, BF16, 1024, 1024, "proj_ret_v")
        rg = _matmul(hb, w_g, F32, 1024, 1024, "proj_ret_g")
        r_out = _retention(log_decay, rqk, rv, rg, B, S)

        gates = _matmul(hb, w_gates, F32, 1024, 1024, "proj_gates")
        mixed = _merge(a_out, r_out, w_br_mla[l].astype(BF16), w_br_ret[l].astype(BF16),
                       gates, gate_bias[l].reshape(1, -1), 512, 1024)
        y = _matmul(mixed, w_o[l].astype(BF16), F32, 1024, 1024, "proj_o")
        h, hb = _res_ln(y, h, ln1_g[l][None, :], ln1_b[l][None, :], 256, True)

        w_ckv = jnp.concatenate([w_ck[l], w_cv[l]], axis=1).astype(BF16)
        ckv = _matmul(memb, w_ckv, BF16, 1024, 1024, "proj_mem_kv")
        c_out = _xattn(hb, w_cq[l].astype(BF16), ckv, S, mem_len, 1024)
        y = _matmul(c_out, w_co[l].astype(BF16), F32, 1024, 1024, "proj_co")
        h, hb = _res_ln(y, h, ln2_g[l][None, :], ln2_b[l][None, :], 256, True)

        act = _ffn_in(hb, wfi, cw, cb, S, 512, 512)
        y = _ffn_out(act, wfo, 512, 1024)
        if l == DEPTH - 1:
            (h,) = _res_ln(y, h, ln3_g[l][None, :], ln3_b[l][None, :], 256, False)
        else:
            h, hb = _res_ln(y, h, ln3_g[l][None, :], ln3_b[l][None, :], 256, True)
    return h.reshape(B, S, D)
```
